```python
import jax
import jax.numpy as jnp
from jax import lax
import numpy as np


D_MODEL = 1024
BATCH = 4
SEQ = 8192
DEPTH = 4

N_MIXERS = 3
CHUNK = 64
NORM_EPS = 1e-6
RET_HEADS = 4
RET_QK_DIM = D_MODEL // RET_HEADS
RET_V_DIM = 2 * D_MODEL // RET_HEADS
RET_IN = 2 * RET_HEADS * RET_QK_DIM + 2 * RET_HEADS * RET_V_DIM
ROPE_BASE = 10000.0
CONV_WIDTH = 31
GLA_HEADS = 4
GLA_K_DIM = D_MODEL // 2 // GLA_HEADS
GLA_V_DIM = D_MODEL // GLA_HEADS
GLA_IN = 2 * GLA_HEADS * GLA_K_DIM + 2 * GLA_HEADS * GLA_V_DIM
GLA_GATE_RANK = 16
GLA_GATE_NORMALIZER = 16.0
D_FF = 4 * D_MODEL
N_RET = (DEPTH + 2) // 3
N_CONV = (DEPTH + 1) // 3
N_GLA = DEPTH // 3

F32 = jnp.float32

kernel_name = "hybrid_retention_conformer_gla_encoder"


def rmsnorm(x, gain=None):
    xf = x.astype(F32)
    y = xf * lax.rsqrt(jnp.mean(xf * xf, axis=-1, keepdims=True) + NORM_EPS)
    if gain is not None:
        y = y * gain.astype(F32)
    return y.astype(x.dtype)


def layernorm(x, gain, bias):
    xf = x.astype(F32)
    mu = jnp.mean(xf, axis=-1, keepdims=True)
    var = jnp.mean(jnp.square(xf - mu), axis=-1, keepdims=True)
    y = (xf - mu) * lax.rsqrt(var + NORM_EPS) * gain.astype(F32) + bias.astype(F32)
    return y.astype(x.dtype)


def split_heads(t, n):
    b, l, _ = t.shape
    return t.reshape(b, l, n, -1).transpose(0, 2, 1, 3)


def merge_heads(t):
    b, n, l, d = t.shape
    return t.transpose(0, 2, 1, 3).reshape(b, l, n * d)


def rotary(t, positions):
    d = t.shape[-1]
    half = d // 2
    inv_freq = ROPE_BASE ** (-jnp.arange(half, dtype=F32) / half)
    ang = positions.astype(F32)[:, None, :, None] * inv_freq
    cos, sin = jnp.cos(ang), jnp.sin(ang)
    tf = t.astype(F32)
    t1, t2 = tf[..., :half], tf[..., half:]
    return jnp.concatenate([t1 * cos - t2 * sin, t2 * cos + t1 * sin], axis=-1).astype(t.dtype)


def gated_linear_scan(q, k, v, log_g, strict):
    dtype = v.dtype
    bsz, nh, seq, dk = q.shape
    dv = v.shape[-1]
    n = seq // CHUNK
    q = q.astype(F32).reshape(bsz, nh, n, CHUNK, dk)
    k = k.astype(F32).reshape(bsz, nh, n, CHUNK, dk)
    v = v.astype(F32).reshape(bsz, nh, n, CHUNK, dv)
    lg = log_g.astype(F32)
    lg = lg.reshape(lg.shape[0], nh, n, CHUNK, lg.shape[-1])
    c = jnp.cumsum(lg, axis=3)
    c_last = c[:, :, :, -1:, :]
    q_dec = q * jnp.exp(c)
    k_dec = k * jnp.exp(-c)
    k_end = k * jnp.exp(c_last - c)
    mask = jnp.tril(jnp.ones((CHUNK, CHUNK), dtype=bool), k=-1 if strict else 0)
    scores = jnp.einsum("bhncd,bhnsd->bhncs", q_dec, k_dec)
    o_intra = jnp.einsum("bhncs,bhnsv->bhncv", jnp.where(mask, scores, 0.0), v)
    chunk_decay = jnp.exp(c_last[:, :, :, 0, :])

    def step(state, inp):
        qd, ke, vv, dec = inp
        o = jnp.einsum("bhcd,bhdv->bhcv", qd, state)
        state = dec[..., None] * state + jnp.einsum("bhcd,bhcv->bhdv", ke, vv)
        return state, o

    state0 = jnp.zeros((bsz, nh, dk, dv), F32)
    xs = (jnp.moveaxis(q_dec, 2, 0), jnp.moveaxis(k_end, 2, 0),
          jnp.moveaxis(v, 2, 0), jnp.moveaxis(chunk_decay, 2, 0))
    _, o_inter = lax.scan(step, state0, xs)
    o = o_intra + jnp.moveaxis(o_inter, 0, 2)
    return o.reshape(bsz, nh, seq, dv).astype(dtype)


def bidirectional_scan(q, k, v, log_g_fwd, log_g_bwd):
    fwd = gated_linear_scan(q, k, v, log_g_fwd, strict=False)
    rev = lambda t: jnp.flip(t, axis=2)
    bwd = rev(gated_linear_scan(rev(q), rev(k), rev(v), rev(log_g_bwd), strict=True))
    return fwd + bwd


def retention(u, positions, w_in, decay_logit, w_out):
    seq = u.shape[1]
    qk = RET_HEADS * RET_QK_DIM
    vd = RET_HEADS * RET_V_DIM
    q, k, v, g = jnp.split(u @ w_in, [qk, 2 * qk, 2 * qk + vd], axis=-1)
    q = rotary(split_heads(q, RET_HEADS), positions) * (RET_QK_DIM ** -0.5)
    k = rotary(split_heads(k, RET_HEADS), positions)
    v = split_heads(v, RET_HEADS)
    log_gamma = jax.nn.log_sigmoid(decay_logit.astype(F32))
    lg = lambda d: jnp.broadcast_to(log_gamma[d][None, :, None, None], (1, RET_HEADS, seq, 1))
    o = bidirectional_scan(q, k, v, lg(0), lg(1))
    o = merge_heads(rmsnorm(o)) * jax.nn.silu(g)
    return o @ w_out


def conformer_conv(u, w_in, b_in, w_dw, b_dw, ln_gain, ln_bias, w_out, b_out):
    a, gate = jnp.split(u @ w_in + b_in, 2, axis=-1)
    h = a * jax.nn.sigmoid(gate)
    pad = CONV_WIDTH // 2
    h = lax.conv_general_dilated(
        h, w_dw[:, None, :].astype(h.dtype), window_strides=(1,), padding=[(pad, pad)],
        dimension_numbers=("NWC", "WIO", "NWC"), feature_group_count=D_MODEL) + b_dw
    h = jax.nn.silu(layernorm(h, ln_gain, ln_bias))
    return h @ w_out + b_out


def gla(u, w_in, gate_w1, gate_w2, gate_b, norm_gain, w_out):
    kd = GLA_HEADS * GLA_K_DIM
    vd = GLA_HEADS * GLA_V_DIM
    q, k, v, r = jnp.split(u @ w_in, [kd, 2 * kd, 2 * kd + vd], axis=-1)
    q = split_heads(q, GLA_HEADS) * (GLA_K_DIM ** -0.5)
    k = split_heads(k, GLA_HEADS)
    v = split_heads(v, GLA_HEADS)

    def log_gate(d):
        logits = (u @ gate_w1[d]) @ gate_w2[d] + gate_b[d]
        return split_heads(jax.nn.log_sigmoid(logits.astype(F32)) / GLA_GATE_NORMALIZER, GLA_HEADS)

    o = bidirectional_scan(q, k, v, log_gate(0), log_gate(1))
    o = merge_heads(rmsnorm(o, norm_gain)) * jax.nn.silu(r)
    return o @ w_out


def sq_relu_mlp(u, w_up, w_down):
    return jnp.square(jax.nn.relu(u @ w_up)) @ w_down


def setup_inputs(seed: int = 0) -> dict:
    key = jax.random.key(seed)
    ks = jax.random.split(key, 24)
    nrm = lambda k, shape, scale: jax.random.normal(k, shape, F32) * scale
    x = jax.random.normal(ks[0], (BATCH, SEQ, D_MODEL), F32)
    positions = (jnp.arange(SEQ, dtype=jnp.int32)[None, :]
                 + jax.random.randint(ks[1], (BATCH, 1), 0, SEQ, dtype=jnp.int32))
    norm_gains = 1.0 + nrm(ks[2], (DEPTH, 4, D_MODEL), 0.1)
    ret_w_in = nrm(ks[3], (N_RET, D_MODEL, RET_IN), D_MODEL ** -0.5)
    a = 5.0 + jnp.arange(RET_HEADS, dtype=F32)
    base_logit = jnp.log(2.0 ** a - 1.0)
    ret_decay_logit = base_logit + nrm(ks[4], (N_RET, 2, RET_HEADS), 0.05)
    ret_w_out = nrm(ks[5], (N_RET, RET_HEADS * RET_V_DIM, D_MODEL), (RET_HEADS * RET_V_DIM) ** -0.5)
    conv_w_in = nrm(ks[6], (N_CONV, D_MODEL, 2 * D_MODEL), D_MODEL ** -0.5)
    conv_b_in = nrm(ks[7], (N_CONV, 2 * D_MODEL), 0.02)
    conv_w_dw = nrm(ks[8], (N_CONV, CONV_WIDTH, D_MODEL), CONV_WIDTH ** -0.5)
    conv_b_dw = nrm(ks[9], (N_CONV, D_MODEL), 0.02)
    conv_ln_gain = 1.0 + nrm(ks[10], (N_CONV, D_MODEL), 0.1)
    conv_ln_bias = nrm(ks[11], (N_CONV, D_MODEL), 0.02)
    conv_w_out = nrm(ks[12], (N_CONV, D_MODEL, D_MODEL), D_MODEL ** -0.5)
    conv_b_out = nrm(ks[13], (N_CONV, D_MODEL), 0.02)
    gla_w_in = nrm(ks[14], (N_GLA, D_MODEL, GLA_IN), D_MODEL ** -0.5)
    gla_gate_w1 = nrm(ks[15], (N_GLA, 2, D_MODEL, GLA_GATE_RANK), D_MODEL ** -0.5)
    gla_gate_w2 = nrm(ks[16], (N_GLA, 2, GLA_GATE_RANK, GLA_HEADS * GLA_K_DIM), GLA_GATE_RANK ** -0.5)
    gla_gate_b = nrm(ks[17], (N_GLA, 2, GLA_HEADS * GLA_K_DIM), 0.1)
    gla_norm_gain = 1.0 + nrm(ks[18], (N_GLA, GLA_V_DIM), 0.1)
    gla_w_out = nrm(ks[19], (N_GLA, GLA_HEADS * GLA_V_DIM, D_MODEL), (GLA_HEADS * GLA_V_DIM) ** -0.5)
    mlp_w_up = nrm(ks[20], (DEPTH, D_MODEL, D_FF), D_MODEL ** -0.5)
    mlp_w_down = nrm(ks[21], (DEPTH, D_FF, D_MODEL), D_FF ** -0.5)
    return {
        "x": x, "positions": positions, "norm_gains": norm_gains,
        "ret_w_in": ret_w_in, "ret_decay_logit": ret_decay_logit, "ret_w_out": ret_w_out,
        "conv_w_in": conv_w_in, "conv_b_in": conv_b_in, "conv_w_dw": conv_w_dw,
        "conv_b_dw": conv_b_dw, "conv_ln_gain": conv_ln_gain, "conv_ln_bias": conv_ln_bias,
        "conv_w_out": conv_w_out, "conv_b_out": conv_b_out,
        "gla_w_in": gla_w_in, "gla_gate_w1": gla_gate_w1, "gla_gate_w2": gla_gate_w2,
        "gla_gate_b": gla_gate_b, "gla_norm_gain": gla_norm_gain, "gla_w_out": gla_w_out,
        "mlp_w_up": mlp_w_up, "mlp_w_down": mlp_w_down,
    }


def reference(x, positions, norm_gains, ret_w_in, ret_decay_logit, ret_w_out,
              conv_w_in, conv_b_in, conv_w_dw, conv_b_dw, conv_ln_gain, conv_ln_bias,
              conv_w_out, conv_b_out, gla_w_in, gla_gate_w1, gla_gate_w2, gla_gate_b,
              gla_norm_gain, gla_w_out, mlp_w_up, mlp_w_down):
    h = x
    for i in range(DEPTH):
        kind = i % N_MIXERS
        j = i // N_MIXERS
        u = rmsnorm(h, norm_gains[i, 0])
        if kind == 0:
            y = retention(u, positions, ret_w_in[j], ret_decay_logit[j], ret_w_out[j])
        elif kind == 1:
            y = conformer_conv(u, conv_w_in[j], conv_b_in[j], conv_w_dw[j], conv_b_dw[j],
                               conv_ln_gain[j], conv_ln_bias[j], conv_w_out[j], conv_b_out[j])
        else:
            y = gla(u, gla_w_in[j], gla_gate_w1[j], gla_gate_w2[j], gla_gate_b[j],
                    gla_norm_gain[j], gla_w_out[j])
        h = h + rmsnorm(y, norm_gains[i, 1])
        u = rmsnorm(h, norm_gains[i, 2])
        h = h + rmsnorm(sq_relu_mlp(u, mlp_w_up[i], mlp_w_down[i]), norm_gains[i, 3])
    return h
```

```python
import functools

import jax
import jax.numpy as jnp
from jax import lax
from jax.experimental import pallas as pl
from jax.experimental.pallas import tpu as pltpu

F32 = jnp.float32
BF16 = jnp.bfloat16

NORM_EPS = 1e-6
ROPE_BASE = 10000.0
CHUNK = 64
RET_HEADS = 4
GLA_HEADS = 4
CONV_WIDTH = 31
GLA_GATE_NORMALIZER = 16.0
HALO = 16
SUBLANES = 8

VMEM_LIMIT_BYTES = 56 * 1024 * 1024


def _params(*semantics):
    return pltpu.CompilerParams(dimension_semantics=semantics, vmem_limit_bytes=VMEM_LIMIT_BYTES)


def _rms(x, gain):
    ms = jnp.mean(x * x, axis=-1, keepdims=True)
    return x * lax.rsqrt(ms + NORM_EPS) * gain


def _log_sigmoid(x):
    return jnp.minimum(x, 0.0) - jnp.log1p(jnp.exp(-jnp.abs(x)))


def _dot(a, b):
    return jnp.dot(a, b, preferred_element_type=F32)


def _dot_nt(a, b):
    return lax.dot_general(a, b, (((1,), (1,)), ((), ())), preferred_element_type=F32)


def _dot_tn(a, b):
    return lax.dot_general(a, b, (((0,), (0,)), ((), ())), preferred_element_type=F32)


def _rope_kernel(pos_ref, invf_ref, cos_ref, sin_ref):
    ang = pos_ref[...].astype(F32) * invf_ref[...]
    cos_ref[...] = jnp.cos(ang)
    sin_ref[...] = jnp.sin(ang)


def _rope_tables(positions, half):
    t = positions.size
    tm = min(2048, t)
    inv_freq = (ROPE_BASE ** (-jnp.arange(half, dtype=F32) / half)).reshape(1, half)
    pos = positions.reshape(t, 1)
    return pl.pallas_call(
        _rope_kernel,
        grid=(t // tm,),
        in_specs=[pl.BlockSpec((tm, 1), lambda i: (i, 0)),
                  pl.BlockSpec((1, half), lambda i: (0, 0))],
        out_specs=[pl.BlockSpec((tm, half), lambda i: (i, 0)),
                   pl.BlockSpec((tm, half), lambda i: (i, 0))],
        out_shape=[jax.ShapeDtypeStruct((t, half), F32)] * 2,
        compiler_params=_params("arbitrary"),
        name="rope_tables",
    )(pos, inv_freq)


def _ret_inproj_kernel(h_ref, g_ref, w_ref, cos_ref, sin_ref, o_ref, u_ref, *, heads, dk):
    j = pl.program_id(1)

    @pl.when(j == 0)
    def _():
        u_ref[...] = _rms(h_ref[...], g_ref[...]).astype(BF16)

    y = _dot(u_ref[...], w_ref[...])
    half = dk // 2

    @pl.when(j < 2)
    def _():
        scale = jnp.where(j == 0, dk ** -0.5, 1.0).astype(F32)
        cos = cos_ref[...]
        sin = sin_ref[...]
        for hh in range(heads):
            t1 = y[:, hh * dk: hh * dk + half]
            t2 = y[:, hh * dk + half: (hh + 1) * dk]
            o_ref[:, hh * dk: hh * dk + half] = ((t1 * cos - t2 * sin) * scale).astype(BF16)
            o_ref[:, hh * dk + half: (hh + 1) * dk] = ((t2 * cos + t1 * sin) * scale).astype(BF16)

    @pl.when(j >= 2)
    def _():
        o_ref[...] = y.astype(BF16)


def _ret_inproj(h, gain, w_in, cos, sin, heads, dk):
    t, d = h.shape
    n = w_in.shape[1]
    tn = heads * dk
    tm = min(1024, t)
    return pl.pallas_call(
        functools.partial(_ret_inproj_kernel, heads=heads, dk=dk),
        grid=(t // tm, n // tn),
        in_specs=[pl.BlockSpec((tm, d), lambda i, j: (i, 0)),
                  pl.BlockSpec((1, d), lambda i, j: (0, 0)),
                  pl.BlockSpec((d, tn), lambda i, j: (0, j)),
                  pl.BlockSpec((tm, dk // 2), lambda i, j: (i, 0)),
                  pl.BlockSpec((tm, dk // 2), lambda i, j: (i, 0))],
        out_specs=pl.BlockSpec((tm, tn), lambda i, j: (i, j)),
        out_shape=jax.ShapeDtypeStruct((t, n), BF16),
        scratch_shapes=[pltpu.VMEM((tm, d), BF16)],
        compiler_params=_params("arbitrary", "arbitrary"),
        name="ret_inproj",
    )(h, gain, w_in, cos, sin)


def _ret_scan_kernel(dl_ref, q_ref, k_ref, v_ref, g_ref, o_ref,
                     bst_ref, fstate_ref, bstate_ref, dmat_ref, *, nb, tb):
    t = pl.program_id(2)
    lg = _log_sigmoid(dl_ref[0])
    lgf = lg[0:1, 0:1]
    lgb = lg[1:2, 0:1]
    il = lax.broadcasted_iota(jnp.int32, (tb, 1), 0).astype(F32)

    @pl.when(t == 0)
    def _():
        bstate_ref[...] = jnp.zeros_like(bstate_ref)
        fstate_ref[...] = jnp.zeros_like(fstate_ref)
        ii = lax.broadcasted_iota(jnp.int32, (tb, tb), 0)
        jj = lax.broadcasted_iota(jnp.int32, (tb, tb), 1)
        dist = (ii - jj).astype(F32)
        dmat_ref[...] = jnp.exp(jnp.where(dist >= 0, dist * lgf, -dist * lgb))

    @pl.when(t < nb)
    def _():
        n = nb - 1 - t
        bst_ref[n] = bstate_ref[...].astype(BF16)
        kd = (k_ref[0].astype(F32) * jnp.exp(il * lgb)).astype(BF16)
        bstate_ref[...] = bstate_ref[...] * jnp.exp(tb * lgb) + _dot_tn(kd, v_ref[0])

    @pl.when(t >= nb)
    def _():
        n = t - nb
        q = q_ref[0]
        k = k_ref[0]
        v = v_ref[0]
        p = (_dot_nt(q, k) * dmat_ref[...]).astype(BF16)
        o = _dot(p, v)
        o = o + jnp.exp((il + 1.0) * lgf) * _dot(q, fstate_ref[...].astype(BF16))
        o = o + jnp.exp((tb - il) * lgb) * _dot(q, bst_ref[n])
        kd = (k.astype(F32) * jnp.exp((tb - 1.0 - il) * lgf)).astype(BF16)
        fstate_ref[...] = fstate_ref[...] * jnp.exp(tb * lgf) + _dot_tn(kd, v)
        ms = jnp.mean(o * o, axis=-1, keepdims=True)
        g = g_ref[0].astype(F32)
        o_ref[0] = (o * lax.rsqrt(ms + NORM_EPS) * (g * jax.nn.sigmoid(g))).astype(BF16)


def _ret_scan(qkvg, decay_logit, bsz, seq, heads, dk, dv):
    tb = min(512, seq)
    nb = seq // tb
    x = qkvg.reshape(bsz, seq, qkvg.shape[-1])
    dl = jnp.broadcast_to(decay_logit.astype(F32).T[:, :, None], (heads, 2, 128))
    kv_blk = lambda b, h, t: jnp.where(t < nb, nb - 1 - t, t - nb)
    q_blk = lambda b, h, t: jnp.maximum(t - nb, 0)
    qo, ko, vo, go = 0, heads * dk // dk, 2 * heads * dk // dv, (2 * heads * dk + heads * dv) // dv
    return pl.pallas_call(
        functools.partial(_ret_scan_kernel, nb=nb, tb=tb),
        grid=(bsz, heads, 2 * nb),
        in_specs=[pl.BlockSpec((1, 2, 128), lambda b, h, t: (h, 0, 0)),
                  pl.BlockSpec((1, tb, dk), lambda b, h, t: (b, q_blk(b, h, t), qo + h)),
                  pl.BlockSpec((1, tb, dk), lambda b, h, t: (b, kv_blk(b, h, t), ko + h)),
                  pl.BlockSpec((1, tb, dv), lambda b, h, t: (b, kv_blk(b, h, t), vo + h)),
                  pl.BlockSpec((1, tb, dv), lambda b, h, t: (b, q_blk(b, h, t), go + h))],
        out_specs=pl.BlockSpec((1, tb, dv), lambda b, h, t: (b, q_blk(b, h, t), h)),
        out_shape=jax.ShapeDtypeStruct((bsz, seq, heads * dv), BF16),
        scratch_shapes=[pltpu.VMEM((nb, dk, dv), BF16),
                        pltpu.VMEM((dk, dv), F32),
                        pltpu.VMEM((dk, dv), F32),
                        pltpu.VMEM((tb, tb), F32)],
        compiler_params=_params("arbitrary", "arbitrary", "arbitrary"),
        name="ret_scan",
    )(dl, x, x, x, x).reshape(bsz * seq, heads * dv)


def _outproj_kernel(x_ref, w_ref, g_ref, h_ref, o_ref):
    y = _dot(x_ref[...], w_ref[...])
    o_ref[...] = h_ref[...] + _rms(y, g_ref[...])


def _outproj(x, w, gain, h):
    t, k = x.shape
    d = w.shape[1]
    tm = min(512, t)
    return pl.pallas_call(
        _outproj_kernel,
        grid=(t // tm,),
        in_specs=[pl.BlockSpec((tm, k), lambda i: (i, 0)),
                  pl.BlockSpec((k, d), lambda i: (0, 0)),
                  pl.BlockSpec((1, d), lambda i: (0, 0)),
                  pl.BlockSpec((tm, d), lambda i: (i, 0))],
        out_specs=pl.BlockSpec((tm, d), lambda i: (i, 0)),
        out_shape=jax.ShapeDtypeStruct((t, d), F32),
        compiler_params=_params("arbitrary"),
        name="outproj_norm_res",
    )(x, w, gain, h)


def _conv_inproj_kernel(h_ref, g_ref, w_ref, b_ref, o_ref, *, d):
    u = _rms(h_ref[...], g_ref[...]).astype(BF16)
    y = _dot(u, w_ref[...]) + b_ref[...]
    o_ref[...] = y[:, :d] * jax.nn.sigmoid(y[:, d:])


def _conv_inproj(h, gain, w_in, b_in):
    t, d = h.shape
    tm = min(512, t)
    return pl.pallas_call(
        functools.partial(_conv_inproj_kernel, d=d),
        grid=(t // tm,),
        in_specs=[pl.BlockSpec((tm, d), lambda i: (i, 0)),
                  pl.BlockSpec((1, d), lambda i: (0, 0)),
                  pl.BlockSpec((d, 2 * d), lambda i: (0, 0)),
                  pl.BlockSpec((1, 2 * d), lambda i: (0, 0))],
        out_specs=pl.BlockSpec((tm, d), lambda i: (i, 0)),
        out_shape=jax.ShapeDtypeStruct((t, d), F32),
        compiler_params=_params("arbitrary"),
        name="conv_inproj",
    )(h, gain, w_in, b_in)


def _conv_mix_kernel(prev_ref, cur_ref, next_ref, wdw_ref, bdw_ref, lng_ref, lnb_ref,
                     w_ref, bo_ref, g_ref, h_ref, o_ref, win_ref, sh_ref, acc_ref, *, tl, nl, rows):
    i = pl.program_id(1)
    d = cur_ref.shape[-1]
    win_ref[0:HALO, :] = jnp.where(i > 0, prev_ref[0], 0.0)
    win_ref[HALO:HALO + tl, :] = cur_ref[0]
    win_ref[HALO + tl:2 * HALO + tl, :] = jnp.where(i < nl - 1, next_ref[0], 0.0)
    ns = sh_ref.shape[1]
    for s in range(SUBLANES):
        sh_ref[s] = win_ref[s:s + ns, :]
    off = HALO - CONV_WIDTH // 2

    def strip(r, carry):
        r0 = pl.multiple_of(r * rows, rows)
        for c in range(d // 128):
            cs = slice(c * 128, (c + 1) * 128)
            acc = jnp.zeros((rows, 128), F32)
            for w in range(CONV_WIDTH):
                a, s = divmod(off + w, SUBLANES)
                acc = acc + sh_ref[s, pl.ds(r0 + a * SUBLANES, rows), cs] * wdw_ref[w:w + 1, cs]
            acc_ref[pl.ds(r0, rows), cs] = acc
        return carry

    lax.fori_loop(0, tl // rows, strip, 0)
    y = acc_ref[...] + bdw_ref[...]
    mu = jnp.mean(y, axis=-1, keepdims=True)
    yc = y - mu
    var = jnp.mean(yc * yc, axis=-1, keepdims=True)
    z = yc * lax.rsqrt(var + NORM_EPS) * lng_ref[...] + lnb_ref[...]
    z = (z * jax.nn.sigmoid(z)).astype(BF16)
    out = _dot(z, w_ref[...]) + bo_ref[...]
    o_ref[0] = h_ref[0] + _rms(out, g_ref[...])


def _conv_mix(h1, w_dw, b_dw, ln_g, ln_b, w_out, b_out, gain, h, bsz, seq):
    d = h1.shape[-1]
    tl = min(256, seq)
    nl = seq // tl
    hb = tl // HALO
    taps_span = (HALO - CONV_WIDTH // 2 + CONV_WIDTH - 1) // SUBLANES * SUBLANES
    x = h1.reshape(bsz, seq, d)
    h3 = h.reshape(bsz, seq, d)
    vec = pl.BlockSpec((1, d), lambda b, i: (0, 0))
    return pl.pallas_call(
        functools.partial(_conv_mix_kernel, tl=tl, nl=nl, rows=32),
        grid=(bsz, nl),
        in_specs=[pl.BlockSpec((1, HALO, d), lambda b, i: (b, jnp.maximum(i * hb - 1, 0), 0)),
                  pl.BlockSpec((1, tl, d), lambda b, i: (b, i, 0)),
                  pl.BlockSpec((1, HALO, d), lambda b, i: (b, jnp.minimum((i + 1) * hb, nl * hb - 1), 0)),
                  pl.BlockSpec((CONV_WIDTH, d), lambda b, i: (0, 0)),
                  vec, vec, vec,
                  pl.BlockSpec((d, d), lambda b, i: (0, 0)),
                  vec, vec,
                  pl.BlockSpec((1, tl, d), lambda b, i: (b, i, 0))],
        out_specs=pl.BlockSpec((1, tl, d), lambda b, i: (b, i, 0)),
        out_shape=jax.ShapeDtypeStruct((bsz, seq, d), F32),
        scratch_shapes=[pltpu.VMEM((tl + 2 * HALO, d), F32),
                        pltpu.VMEM((SUBLANES, tl + taps_span, d), F32),
                        pltpu.VMEM((tl, d), F32)],
        compiler_params=_params("arbitrary", "arbitrary"),
        name="conv_mix",
    )(x, x, x, w_dw, b_dw, ln_g, ln_b, w_out, b_out, gain, h3).reshape(bsz * seq, d)


def _gla_inproj_kernel(h_ref, g_ref, w_ref, w1_ref, w2_ref, gb_ref, o_ref, lg_ref, u_ref, *, dk):
    j = pl.program_id(1)

    @pl.when(j == 0)
    def _():
        u = _rms(h_ref[...], g_ref[...]).astype(BF16)
        u_ref[...] = u
        low = _dot(u, w1_ref[...]).astype(BF16)
        logits = _dot(low, w2_ref[...]) + gb_ref[...]
        lg_ref[...] = _log_sigmoid(logits) * (1.0 / GLA_GATE_NORMALIZER)

    y = _dot(u_ref[...], w_ref[...])

    @pl.when(j == 0)
    def _():
        half = y.shape[1] // 2
        o_ref[:, :half] = (y[:, :half] * (dk ** -0.5)).astype(BF16)
        o_ref[:, half:] = y[:, half:].astype(BF16)

    @pl.when(j > 0)
    def _():
        o_ref[...] = y.astype(BF16)


def _gla_inproj(h, gain, w_in, w1cat, w2bd, gbcat, dk):
    t, d = h.shape
    n = w_in.shape[1]
    tn = 1024
    tm = min(1024, t)
    r2 = w1cat.shape[1]
    ng = w2bd.shape[1]
    return pl.pallas_call(
        functools.partial(_gla_inproj_kernel, dk=dk),
        grid=(t // tm, n // tn),
        in_specs=[pl.BlockSpec((tm, d), lambda i, j: (i, 0)),
                  pl.BlockSpec((1, d), lambda i, j: (0, 0)),
                  pl.BlockSpec((d, tn), lambda i, j: (0, j)),
                  pl.BlockSpec((d, r2), lambda i, j: (0, 0)),
                  pl.BlockSpec((r2, ng), lambda i, j: (0, 0)),
                  pl.BlockSpec((1, ng), lambda i, j: (0, 0))],
        out_specs=[pl.BlockSpec((tm, tn), lambda i, j: (i, j)),
                   pl.BlockSpec((tm, ng), lambda i, j: (i, 0))],
        out_shape=[jax.ShapeDtypeStruct((t, n), BF16),
                   jax.ShapeDtypeStruct((t, ng), F32)],
        scratch_shapes=[pltpu.VMEM((tm, d), BF16)],
        compiler_params=_params("arbitrary", "arbitrary"),
        name="gla_inproj",
    )(h, gain, w_in, w1cat, w2bd, gbcat)


def _chunk_cumsum(x, tb):
    row = lax.broadcasted_iota(jnp.int32, x.shape, 0) % CHUNK
    s = 1
    while s < CHUNK:
        x = x + jnp.where(row >= s, pltpu.roll(x, s, axis=0), 0.0)
        s *= 2
    return x


def _col(v):
    n = v.shape[1]
    return jnp.transpose(jnp.broadcast_to(v, (8, n)))[:, 0:1]


def _gla_scan_kernel(q_ref, k_ref, v_ref, r_ref, lgf_ref, lgb_ref, gain_ref, o_ref,
                     bst_ref, fstate_ref, bstate_ref, *, nb, tb):
    t = pl.program_id(2)
    nc = tb // CHUNK

    @pl.when(t == 0)
    def _():
        bstate_ref[...] = jnp.zeros_like(bstate_ref)
        fstate_ref[...] = jnp.zeros_like(fstate_ref)

    @pl.when(t < nb)
    def _():
        n = nb - 1 - t
        lb = lgb_ref[0]
        pb = _chunk_cumsum(lb, tb)
        ke = (k_ref[0].astype(F32) * jnp.exp(pb - lb)).astype(BF16)
        v = v_ref[0]
        for ci in range(nc - 1, -1, -1):
            rs = slice(ci * CHUNK, (ci + 1) * CHUNK)
            bst_ref[n * nc + ci] = bstate_ref[...].astype(BF16)
            tot = pb[(ci + 1) * CHUNK - 1:(ci + 1) * CHUNK, :]
            bstate_ref[...] = bstate_ref[...] * _col(jnp.exp(tot)) + _dot_tn(ke[rs], v[rs])

    @pl.when(t >= nb)
    def _():
        n = t - nb
        q = q_ref[0].astype(F32)
        k = k_ref[0].astype(F32)
        v = v_ref[0]
        lf = lgf_ref[0]
        lb = lgb_ref[0]
        pf = _chunk_cumsum(lf, tb)
        pb = _chunk_cumsum(lb, tb)
        ii = lax.broadcasted_iota(jnp.int32, (CHUNK, CHUNK), 0)
        jj = lax.broadcasted_iota(jnp.int32, (CHUNK, CHUNK), 1)
        lower = ii >= jj
        for ci in range(nc):
            rs = slice(ci * CHUNK, (ci + 1) * CHUNK)
            last = slice((ci + 1) * CHUNK - 1, (ci + 1) * CHUNK)
            pfc, pbc, lbc = pf[rs], pb[rs], lb[rs]
            totf, totb = pf[last], pb[last]
            cb = totb - pbc + lbc
            qc, kc, vc = q[rs], k[rs], v[rs]
            qf = (qc * jnp.exp(pfc)).astype(BF16)
            kf = (kc * jnp.exp(-pfc)).astype(BF16)
            qb = (qc * jnp.exp(cb)).astype(BF16)
            kb = (kc * jnp.exp(-cb)).astype(BF16)
            p = jnp.where(lower, _dot_nt(qf, kf), _dot_nt(qb, kb)).astype(BF16)
            o = _dot(p, vc)
            o = o + _dot(qf, fstate_ref[...].astype(BF16))
            o = o + _dot(qb, bst_ref[n * nc + ci])
            ke = (kc * jnp.exp(totf - pfc)).astype(BF16)
            fstate_ref[...] = fstate_ref[...] * _col(jnp.exp(totf)) + _dot_tn(ke, vc)
            r = r_ref[0, rs, :].astype(F32)
            o_ref[0, rs, :] = (_rms(o, gain_ref[...]) * (r * jax.nn.sigmoid(r))).astype(BF16)


def _gla_scan(qkvr, lg, norm_gain, bsz, seq, heads, dk, dv):
    tb = min(512, seq)
    nb = seq // tb
    x = qkvr.reshape(bsz, seq, qkvr.shape[-1])
    lg3 = lg.reshape(bsz, seq, lg.shape[-1])
    kv_blk = lambda b, h, t: jnp.where(t < nb, nb - 1 - t, t - nb)
    q_blk = lambda b, h, t: jnp.maximum(t - nb, 0)
    ko = heads * dk // dk
    vo = 2 * heads * dk // dv
    ro = (2 * heads * dk + heads * dv) // dv
    return pl.pallas_call(
        functools.partial(_gla_scan_kernel, nb=nb, tb=tb),
        grid=(bsz, heads, 2 * nb),
        in_specs=[pl.BlockSpec((1, tb, dk), lambda b, h, t: (b, q_blk(b, h, t), h)),
                  pl.BlockSpec((1, tb, dk), lambda b, h, t: (b, kv_blk(b, h, t), ko + h)),
                  pl.BlockSpec((1, tb, dv), lambda b, h, t: (b, kv_blk(b, h, t), vo + h)),
                  pl.BlockSpec((1, tb, dv), lambda b, h, t: (b, q_blk(b, h, t), ro + h)),
                  pl.BlockSpec((1, tb, dk), lambda b, h, t: (b, q_blk(b, h, t), h)),
                  pl.BlockSpec((1, tb, dk), lambda b, h, t: (b, kv_blk(b, h, t), heads + h)),
                  pl.BlockSpec((1, dv), lambda b, h, t: (0, 0))],
        out_specs=pl.BlockSpec((1, tb, dv), lambda b, h, t: (b, q_blk(b, h, t), h)),
        out_shape=jax.ShapeDtypeStruct((bsz, seq, heads * dv), BF16),
        scratch_shapes=[pltpu.VMEM((seq // CHUNK, dk, dv), BF16),
                        pltpu.VMEM((dk, dv), F32),
                        pltpu.VMEM((dk, dv), F32)],
        compiler_params=_params("arbitrary", "arbitrary", "arbitrary"),
        name="gla_scan",
    )(x, x, x, x, lg3, lg3, norm_gain).reshape(bsz * seq, heads * dv)


def _mlp_kernel(h_ref, g2_ref, wu_ref, wd_ref, g3_ref, o_ref, u_ref, acc_ref):
    c = pl.program_id(1)

    @pl.when(c == 0)
    def _():
        u_ref[...] = _rms(h_ref[...], g2_ref[...]).astype(BF16)

    a = jnp.maximum(_dot(u_ref[...], wu_ref[...]), 0.0)
    part = _dot((a * a).astype(BF16), wd_ref[...])

    @pl.when(c == 0)
    def _():
        acc_ref[...] = part

    @pl.when(c > 0)
    def _():
        acc_ref[...] += part

    @pl.when(c == pl.num_programs(1) - 1)
    def _():
        o_ref[...] = h_ref[...] + _rms(acc_ref[...], g3_ref[...])


def _mlp(h, g2, w_up, w_down, g3):
    t, d = h.shape
    ff = w_up.shape[1]
    tm = min(512, t)
    tf = 1024
    return pl.pallas_call(
        _mlp_kernel,
        grid=(t // tm, ff // tf),
        in_specs=[pl.BlockSpec((tm, d), lambda i, c: (i, 0)),
                  pl.BlockSpec((1, d), lambda i, c: (0, 0)),
                  pl.BlockSpec((d, tf), lambda i, c: (0, c)),
                  pl.BlockSpec((tf, d), lambda i, c: (c, 0)),
                  pl.BlockSpec((1, d), lambda i, c: (0, 0))],
        out_specs=pl.BlockSpec((tm, d), lambda i, c: (i, 0)),
        out_shape=jax.ShapeDtypeStruct((t, d), F32),
        scratch_shapes=[pltpu.VMEM((tm, d), BF16),
                        pltpu.VMEM((tm, d), F32)],
        compiler_params=_params("arbitrary", "arbitrary"),
        name="mlp",
    )(h, g2, w_up, w_down, g3)


def _retention_layer(h, gains, cos, sin, w_in, decay_logit, w_out, bsz, seq):
    heads = RET_HEADS
    dk = w_out.shape[1] // heads
    dv = w_out.shape[0] // heads
    qkvg = _ret_inproj(h, gains[0:1], w_in.astype(BF16), cos, sin, heads, dk)
    o = _ret_scan(qkvg, decay_logit, bsz, seq, heads, dk, dv)
    return _outproj(o, w_out.astype(BF16), gains[1:2], h)


def _conv_layer(h, gains, w_in, b_in, w_dw, b_dw, ln_g, ln_b, w_out, b_out, bsz, seq):
    row = lambda a: a.reshape(1, -1)
    h1 = _conv_inproj(h, gains[0:1], w_in.astype(BF16), row(b_in))
    return _conv_mix(h1, w_dw, row(b_dw), row(ln_g), row(ln_b), w_out.astype(BF16), row(b_out),
                     gains[1:2], h, bsz, seq)


def _gla_layer(h, gains, w_in, gate_w1, gate_w2, gate_b, norm_gain, w_out, bsz, seq):
    heads = GLA_HEADS
    dv = w_out.shape[0] // heads
    rank = gate_w1.shape[-1]
    nk = gate_w2.shape[-1]
    dk = nk // heads
    w1cat = jnp.concatenate([gate_w1[0], gate_w1[1]], axis=1).astype(BF16)
    zeros = jnp.zeros((rank, nk), gate_w2.dtype)
    w2bd = jnp.concatenate([jnp.concatenate([gate_w2[0], zeros], axis=1),
                            jnp.concatenate([zeros, gate_w2[1]], axis=1)], axis=0).astype(BF16)
    gbcat = gate_b.reshape(1, 2 * nk)
    qkvr, lg = _gla_inproj(h, gains[0:1], w_in.astype(BF16), w1cat, w2bd, gbcat, dk)
    o = _gla_scan(qkvr, lg, norm_gain.reshape(1, dv), bsz, seq, heads, dk, dv)
    return _outproj(o, w_out.astype(BF16), gains[1:2], h)


def kernel(x, positions, norm_gains, ret_w_in, ret_decay_logit, ret_w_out, conv_w_in, conv_b_in, conv_w_dw, conv_b_dw, conv_ln_gain, conv_ln_bias, conv_w_out, conv_b_out, gla_w_in, gla_gate_w1, gla_gate_w2, gla_gate_b, gla_norm_gain, gla_w_out, mlp_w_up, mlp_w_down):
    bsz, seq, d = x.shape
    depth = norm_gains.shape[0]
    h = x.reshape(bsz * seq, d)
    ret_dk = ret_w_out.shape[2] // RET_HEADS
    cos, sin = _rope_tables(positions, ret_dk // 2)
    for i in range(depth):
        kind, j = i % 3, i // 3
        gains = norm_gains[i]
        if kind == 0:
            h = _retention_layer(h, gains, cos, sin, ret_w_in[j], ret_decay_logit[j], ret_w_out[j], bsz, seq)
        elif kind == 1:
            h = _conv_layer(h, gains, conv_w_in[j], conv_b_in[j], conv_w_dw[j], conv_b_dw[j],
                            conv_ln_gain[j], conv_ln_bias[j], conv_w_out[j], conv_b_out[j], bsz, seq)
        else:
            h = _gla_layer(h, gains, gla_w_in[j], gla_gate_w1[j], gla_gate_w2[j], gla_gate_b[j],
                           gla_norm_gain[j], gla_w_out[j], bsz, seq)
        h = _mlp(h, gains[2:3], mlp_w_up[i].astype(BF16), mlp_w_down[i].astype(BF16), gains[3:4])
    return h.reshape(bsz, seq, d)
```

```python
import functools

import jax
import jax.numpy as jnp
from jax import lax
from jax.experimental import pallas as pl
from jax.experimental.pallas import tpu as pltpu

F32 = jnp.float32
BF16 = jnp.bfloat16

NORM_EPS = 1e-6
ROPE_BASE = 10000.0
CHUNK = 64
RET_HEADS = 4
GLA_HEADS = 4
CONV_WIDTH = 31
GLA_GATE_NORMALIZER = 16.0
HALO = 16
TAP_CHAINS = 4
SUBLANES = 8
LANES = 128

VMEM_LIMIT_BYTES = 56 * 1024 * 1024

ROW_BLOCK = 1024
ROW_SPLIT = 4
SCAN_BLOCK = 512
SCAN_ROWS_BWD = 2048
SCAN_ROWS_FWD = 2048


def _params(*semantics):
    return pltpu.CompilerParams(dimension_semantics=semantics, vmem_limit_bytes=VMEM_LIMIT_BYTES)


def _rms(x, gain):
    ms = jnp.mean(x * x, axis=-1, keepdims=True)
    return x * lax.rsqrt(ms + NORM_EPS) * gain


def _log_sigmoid(x):
    return jnp.minimum(x, 0.0) - jnp.log(1.0 + jnp.exp(-jnp.abs(x)))


def _silu(x):
    return x * jax.nn.sigmoid(x)


def _dot(a, b):
    return jnp.dot(a, b, preferred_element_type=F32)


def _dot_nt(a, b):
    return lax.dot_general(a, b, (((1,), (1,)), ((), ())), preferred_element_type=F32)


def _dot_tn(a, b):
    return lax.dot_general(a, b, (((0,), (0,)), ((), ())), preferred_element_type=F32)


def _by_parity(it, lo, hi, body):
    for par in (0, 1):
        pl.when((it >= lo) & (it <= hi) & ((it & 1) == par))(functools.partial(body, par))


def _rope_kernel(pos_ref, invf_ref, cos_ref, sin_ref):
    ang = pos_ref[...].astype(F32) * invf_ref[...]
    cos_ref[...] = jnp.cos(ang)
    sin_ref[...] = jnp.sin(ang)


def _rope_tables(positions, half):
    t = positions.size
    tm = min(2048, t)
    inv_freq = (ROPE_BASE ** (-jnp.arange(half, dtype=F32) / half)).reshape(1, half)
    pos = positions.reshape(t, 1)
    return pl.pallas_call(
        _rope_kernel,
        grid=(t // tm,),
        in_specs=[pl.BlockSpec((tm, 1), lambda i: (i, 0)),
                  pl.BlockSpec((1, half), lambda i: (0, 0))],
        out_specs=[pl.BlockSpec((tm, half), lambda i: (i, 0)),
                   pl.BlockSpec((tm, half), lambda i: (i, 0))],
        out_shape=[jax.ShapeDtypeStruct((t, half), F32)] * 2,
        compiler_params=_params("arbitrary"),
        name="rope_tables",
    )(pos, inv_freq)


def _ret_inproj_kernel(hn_ref, g_ref, w_ref, cos_ref, sin_ref, o_ref, u0_ref, u1_ref, *, ni, ts, dk):
    it = pl.program_id(0)
    j = pl.program_id(1)
    rows = pl.ds(pl.multiple_of(j * ts, ts), ts)
    u_refs = (u0_ref, u1_ref)
    half = dk // 2

    def prenorm(par):
        u_refs[par][rows, :] = _rms(hn_ref[...], g_ref[...]).astype(BF16)

    def project(par):
        y = _dot(u_refs[1 - par][...], w_ref[...])
        cos = cos_ref[...]
        sin = sin_ref[...]
        for base, scale in ((0, dk ** -0.5), (dk, 1.0)):
            t1 = y[:, base:base + half]
            t2 = y[:, base + half:base + dk]
            o_ref[:, base:base + half] = ((t1 * cos - t2 * sin) * scale).astype(BF16)
            o_ref[:, base + half:base + dk] = ((t2 * cos + t1 * sin) * scale).astype(BF16)
        o_ref[:, 2 * dk:] = y[:, 2 * dk:].astype(BF16)

    @pl.when(it == 0)
    def _():
        prenorm(0)

    def step(par):
        prenorm(par)
        project(par)

    _by_parity(it, 1, ni, step)


def _ret_inproj(h, gain, w_heads, cos, sin, heads, dk):
    t, d = h.shape
    n = w_heads.shape[1]
    tn = n // heads
    tm = min(ROW_BLOCK, t)
    ni = t // tm
    ts = tm // heads
    return pl.pallas_call(
        functools.partial(_ret_inproj_kernel, ni=ni, ts=ts, dk=dk),
        grid=(ni + 1, heads),
        in_specs=[pl.BlockSpec((ts, d), lambda it, j: (jnp.minimum(it, ni - 1) * heads + j, 0)),
                  pl.BlockSpec((1, d), lambda it, j: (0, 0)),
                  pl.BlockSpec((d, tn), lambda it, j: (0, j)),
                  pl.BlockSpec((tm, dk // 2), lambda it, j: (jnp.maximum(it - 1, 0), 0)),
                  pl.BlockSpec((tm, dk // 2), lambda it, j: (jnp.maximum(it - 1, 0), 0))],
        out_specs=pl.BlockSpec((tm, tn), lambda it, j: (jnp.maximum(it - 1, 0), jnp.where(it == 0, 0, j))),
        out_shape=jax.ShapeDtypeStruct((t, n), BF16),
        scratch_shapes=[pltpu.VMEM((tm, d), BF16), pltpu.VMEM((tm, d), BF16)],
        compiler_params=_params("arbitrary", "arbitrary"),
        name="ret_inproj",
    )(h, gain, w_heads, cos, sin)


def _ret_scan_kernel(dl_ref, k0_ref, v0_ref, q_ref, k_ref, v_ref, g_ref, o_ref,
                     st_ref, fstate_ref, bstate_ref, dmat_ref, *, n0, tb):
    t = pl.program_id(2)
    dk = q_ref.shape[-1]
    nblk0 = k0_ref.shape[1] // tb
    nblk1 = q_ref.shape[1] // tb
    lg = _log_sigmoid(dl_ref[0])
    lgf = lg[0:1, 0:1]
    lgb = lg[1:2, 0:1]
    il = lax.broadcasted_iota(jnp.int32, (tb, 1), 0).astype(F32)

    @pl.when(t == 0)
    def _():
        bstate_ref[...] = jnp.zeros_like(bstate_ref)
        fstate_ref[...] = jnp.zeros_like(fstate_ref)
        ii = lax.broadcasted_iota(jnp.int32, (tb, tb), 0)
        jj = lax.broadcasted_iota(jnp.int32, (tb, tb), 1)
        dist = (ii - jj).astype(F32)
        dmat_ref[...] = jnp.exp(jnp.where(dist >= 0, dist * lgf, -dist * lgb))

    @pl.when(t < n0)
    def _():
        blk0 = (n0 - 1 - t) * nblk0
        kscale = jnp.exp(il * lgb)
        dec = jnp.exp(tb * lgb)
        for bi in range(nblk0 - 1, -1, -1):
            rs = slice(bi * tb, (bi + 1) * tb)
            st_ref[blk0 + bi, dk:, :] = bstate_ref[...].astype(BF16)
            kd = (k0_ref[0, rs, :].astype(F32) * kscale).astype(BF16)
            bstate_ref[...] = bstate_ref[...] * dec + _dot_tn(kd, v0_ref[0, rs, :])

    @pl.when(t >= n0)
    def _():
        blk0 = (t - n0) * nblk1
        sf = jnp.exp((il + 1.0) * lgf)
        sb = jnp.exp((tb - il) * lgb)
        kscale = jnp.exp((tb - 1.0 - il) * lgf)
        dec = jnp.exp(tb * lgf)
        for bi in range(nblk1):
            rs = slice(bi * tb, (bi + 1) * tb)
            q = q_ref[0, rs, :]
            k = k_ref[0, rs, :]
            v = v_ref[0, rs, :]
            st_ref[blk0 + bi, :dk, :] = fstate_ref[...].astype(BF16)
            q32 = q.astype(F32)
            qs = jnp.concatenate([(q32 * sf).astype(BF16), (q32 * sb).astype(BF16)], axis=1)
            p = (_dot_nt(q, k) * dmat_ref[...]).astype(BF16)
            o = _dot(p, v) + _dot(qs, st_ref[blk0 + bi])
            kd = (k.astype(F32) * kscale).astype(BF16)
            fstate_ref[...] = fstate_ref[...] * dec + _dot_tn(kd, v)
            ms = jnp.mean(o * o, axis=-1, keepdims=True)
            o_ref[0, rs, :] = (o * lax.rsqrt(ms + NORM_EPS) * _silu(g_ref[0, rs, :].astype(F32))).astype(BF16)


def _ret_scan(qkvg, decay_logit, bsz, seq, heads, dk, dv):
    tb = min(SCAN_BLOCK, seq)
    r0 = min(SCAN_ROWS_BWD, seq)
    r1 = min(SCAN_ROWS_FWD, seq)
    n0, n1 = seq // r0, seq // r1
    x = qkvg.reshape(bsz, seq, qkvg.shape[-1])
    dl = jnp.broadcast_to(decay_logit.astype(F32).T[:, :, None], (heads, 2, LANES))
    blk0 = lambda t: jnp.maximum(n0 - 1 - t, 0)
    blk1 = lambda t: jnp.maximum(t - n0, 0)
    per_head = (2 * dk + 2 * dv)
    qc = lambda h: h * (per_head // dk)
    vc = lambda h: h * (per_head // dv) + 2 * dk // dv
    return pl.pallas_call(
        functools.partial(_ret_scan_kernel, n0=n0, tb=tb),
        grid=(bsz, heads, n0 + n1),
        in_specs=[pl.BlockSpec((1, 2, LANES), lambda b, h, t: (h, 0, 0)),
                  pl.BlockSpec((1, r0, dk), lambda b, h, t: (b, blk0(t), qc(h) + 1)),
                  pl.BlockSpec((1, r0, dv), lambda b, h, t: (b, blk0(t), vc(h))),
                  pl.BlockSpec((1, r1, dk), lambda b, h, t: (b, blk1(t), qc(h))),
                  pl.BlockSpec((1, r1, dk), lambda b, h, t: (b, blk1(t), qc(h) + 1)),
                  pl.BlockSpec((1, r1, dv), lambda b, h, t: (b, blk1(t), vc(h))),
                  pl.BlockSpec((1, r1, dv), lambda b, h, t: (b, blk1(t), vc(h) + 1))],
        out_specs=pl.BlockSpec((1, r1, dv), lambda b, h, t: (b, blk1(t), h)),
        out_shape=jax.ShapeDtypeStruct((bsz, seq, heads * dv), BF16),
        scratch_shapes=[pltpu.VMEM((seq // tb, 2 * dk, dv), BF16),
                        pltpu.VMEM((dk, dv), F32),
                        pltpu.VMEM((dk, dv), F32),
                        pltpu.VMEM((tb, tb), F32)],
        compiler_params=_params("arbitrary", "arbitrary", "arbitrary"),
        name="ret_scan",
    )(dl, x, x, x, x, x, x).reshape(bsz * seq, heads * dv)


def _outproj_kernel(x_ref, w_ref, g_ref, h_ref, o_ref, y0_ref, y1_ref, *, ni):
    it = pl.program_id(0)
    y_refs = (y0_ref, y1_ref)

    def project(par):
        y_refs[par][...] = _dot(x_ref[...], w_ref[...])

    def finish(par):
        o_ref[...] = h_ref[...] + _rms(y_refs[1 - par][...], g_ref[...])

    @pl.when(it == 0)
    def _():
        project(0)

    def step(par):
        project(par)
        finish(par)

    _by_parity(it, 1, ni - 1, step)
    pl.when(it == ni)(functools.partial(finish, ni & 1))


def _outproj(x, w, gain, h):
    t, k = x.shape
    d = w.shape[1]
    tm = min(512, t)
    ni = t // tm
    return pl.pallas_call(
        functools.partial(_outproj_kernel, ni=ni),
        grid=(ni + 1,),
        in_specs=[pl.BlockSpec((tm, k), lambda it: (jnp.minimum(it, ni - 1), 0)),
                  pl.BlockSpec((k, d), lambda it: (0, 0)),
                  pl.BlockSpec((1, d), lambda it: (0, 0)),
                  pl.BlockSpec((tm, d), lambda it: (jnp.maximum(it - 1, 0), 0))],
        out_specs=pl.BlockSpec((tm, d), lambda it: (jnp.maximum(it - 1, 0), 0)),
        out_shape=jax.ShapeDtypeStruct((t, d), F32),
        scratch_shapes=[pltpu.VMEM((tm, d), F32), pltpu.VMEM((tm, d), F32)],
        compiler_params=_params("arbitrary"),
        name="outproj_norm_res",
    )(x, w, gain, h)


def _conv_inproj_kernel(h_ref, g_ref, w_ref, b_ref, o_ref, *, d):
    u = _rms(h_ref[...], g_ref[...]).astype(BF16)
    y = _dot(u, w_ref[...]) + b_ref[...]
    o_ref[...] = y[:, :d] * jax.nn.sigmoid(y[:, d:])


def _conv_inproj(h, gain, w_in, b_in):
    t, d = h.shape
    tm = min(512, t)
    return pl.pallas_call(
        functools.partial(_conv_inproj_kernel, d=d),
        grid=(t // tm,),
        in_specs=[pl.BlockSpec((tm, d), lambda i: (i, 0)),
                  pl.BlockSpec((1, d), lambda i: (0, 0)),
                  pl.BlockSpec((d, 2 * d), lambda i: (0, 0)),
                  pl.BlockSpec((1, 2 * d), lambda i: (0, 0))],
        out_specs=pl.BlockSpec((tm, d), lambda i: (i, 0)),
        out_shape=jax.ShapeDtypeStruct((t, d), F32),
        compiler_params=_params("arbitrary"),
        name="conv_inproj",
    )(h, gain, w_in, b_in)


def _conv_mix_kernel(prev_ref, cur_ref, next_ref, wdw_ref, bdw_ref, lng_ref, lnb_ref,
                     w_ref, bo_ref, g_ref, h_ref, o_ref, win_ref, sh_ref, acc_ref, *, tl, nl, rows):
    i = pl.program_id(1)
    d = cur_ref.shape[-1]
    win_ref[0:HALO, :] = jnp.where(i > 0, prev_ref[0], 0.0)
    win_ref[HALO:HALO + tl, :] = cur_ref[0]
    win_ref[HALO + tl:2 * HALO + tl, :] = jnp.where(i < nl - 1, next_ref[0], 0.0)
    ns = sh_ref.shape[1]
    for s in range(SUBLANES):
        sh_ref[s, :, 0:d] = win_ref[s:s + ns, :]
    off = HALO - CONV_WIDTH // 2

    for c in range(d // LANES):
        cs = slice(c * LANES, (c + 1) * LANES)

        def strip(r, carry, cs=cs):
            r0 = pl.multiple_of(r * rows, rows)
            parts = [None] * TAP_CHAINS
            for w in range(CONV_WIDTH):
                a, s = divmod(off + w, SUBLANES)
                term = sh_ref[s, pl.ds(r0 + a * SUBLANES, rows), cs] * wdw_ref[w:w + 1, cs]
                parts[w % TAP_CHAINS] = term if parts[w % TAP_CHAINS] is None else parts[w % TAP_CHAINS] + term
            acc_ref[pl.ds(r0, rows), cs] = (parts[0] + parts[1]) + (parts[2] + parts[3])
            return carry

        lax.fori_loop(0, tl // rows, strip, 0)
    y = acc_ref[...] + bdw_ref[...]
    mu = jnp.mean(y, axis=-1, keepdims=True)
    yc = y - mu
    var = jnp.mean(yc * yc, axis=-1, keepdims=True)
    z = yc * lax.rsqrt(var + NORM_EPS) * lng_ref[...] + lnb_ref[...]
    out = _dot(_silu(z).astype(BF16), w_ref[...]) + bo_ref[...]
    o_ref[0] = h_ref[0] + _rms(out, g_ref[...])


def _conv_mix(h1, w_dw, b_dw, ln_g, ln_b, w_out, b_out, gain, h, bsz, seq):
    d = h1.shape[-1]
    tl = min(256, seq)
    nl = seq // tl
    hb = tl // HALO
    taps_span = (HALO - CONV_WIDTH // 2 + CONV_WIDTH - 1) // SUBLANES * SUBLANES
    x = h1.reshape(bsz, seq, d)
    h3 = h.reshape(bsz, seq, d)
    vec = pl.BlockSpec((1, d), lambda b, i: (0, 0))
    return pl.pallas_call(
        functools.partial(_conv_mix_kernel, tl=tl, nl=nl, rows=64),
        grid=(bsz, nl),
        in_specs=[pl.BlockSpec((1, HALO, d), lambda b, i: (b, jnp.maximum(i * hb - 1, 0), 0)),
                  pl.BlockSpec((1, tl, d), lambda b, i: (b, i, 0)),
                  pl.BlockSpec((1, HALO, d), lambda b, i: (b, jnp.minimum((i + 1) * hb, nl * hb - 1), 0)),
                  pl.BlockSpec((CONV_WIDTH, d), lambda b, i: (0, 0)),
                  vec, vec, vec,
                  pl.BlockSpec((d, d), lambda b, i: (0, 0)),
                  vec, vec,
                  pl.BlockSpec((1, tl, d), lambda b, i: (b, i, 0))],
        out_specs=pl.BlockSpec((1, tl, d), lambda b, i: (b, i, 0)),
        out_shape=jax.ShapeDtypeStruct((bsz, seq, d), F32),
        scratch_shapes=[pltpu.VMEM((tl + 2 * HALO, d), F32),
                        pltpu.VMEM((SUBLANES, tl + taps_span, d + LANES), F32),
                        pltpu.VMEM((tl, d), F32)],
        compiler_params=_params("arbitrary", "arbitrary"),
        name="conv_mix",
    )(x, x, x, w_dw, b_dw, ln_g, ln_b, w_out, b_out, gain, h3).reshape(bsz * seq, d)


def _gla_inproj_kernel(hn_ref, g_ref, w_ref, cs_ref, w1_ref, w2_ref, gb_ref, o_ref, lg_ref,
                       u0_ref, u1_ref, *, ni, ts):
    it = pl.program_id(0)
    j = pl.program_id(1)
    rows = pl.ds(pl.multiple_of(j * ts, ts), ts)
    u_refs = (u0_ref, u1_ref)

    def prenorm(par):
        u = _rms(hn_ref[...], g_ref[...]).astype(BF16)
        u_refs[par][rows, :] = u
        low = _dot(u, w1_ref[...]).astype(BF16)
        logits = _dot(low, w2_ref[...]) + gb_ref[...]
        lg_ref[...] = _log_sigmoid(logits) * (1.0 / GLA_GATE_NORMALIZER)

    def project(par):
        o_ref[...] = (_dot(u_refs[1 - par][...], w_ref[...]) * cs_ref[...]).astype(BF16)

    @pl.when(it == 0)
    def _():
        prenorm(0)

    def step(par):
        prenorm(par)
        project(par)

    _by_parity(it, 1, ni, step)


def _gla_inproj(h, gain, w_in, colscale, w1cat, w2bd, gbcat):
    t, d = h.shape
    n = w_in.shape[1]
    nj = ROW_SPLIT
    tn = n // nj
    tm = min(ROW_BLOCK, t)
    ni = t // tm
    ts = tm // nj
    r2 = w1cat.shape[1]
    ng = w2bd.shape[1]
    return pl.pallas_call(
        functools.partial(_gla_inproj_kernel, ni=ni, ts=ts),
        grid=(ni + 1, nj),
        in_specs=[pl.BlockSpec((ts, d), lambda it, j: (jnp.minimum(it, ni - 1) * nj + j, 0)),
                  pl.BlockSpec((1, d), lambda it, j: (0, 0)),
                  pl.BlockSpec((d, tn), lambda it, j: (0, j)),
                  pl.BlockSpec((1, tn), lambda it, j: (0, j)),
                  pl.BlockSpec((d, r2), lambda it, j: (0, 0)),
                  pl.BlockSpec((r2, ng), lambda it, j: (0, 0)),
                  pl.BlockSpec((1, ng), lambda it, j: (0, 0))],
        out_specs=[pl.BlockSpec((tm, tn), lambda it, j: (jnp.maximum(it - 1, 0), jnp.where(it == 0, 0, j))),
                   pl.BlockSpec((ts, ng), lambda it, j: (jnp.where(it < ni, it * nj + j, ni * nj - 1), 0))],
        out_shape=[jax.ShapeDtypeStruct((t, n), BF16),
                   jax.ShapeDtypeStruct((t, ng), F32)],
        scratch_shapes=[pltpu.VMEM((tm, d), BF16), pltpu.VMEM((tm, d), BF16)],
        compiler_params=_params("arbitrary", "arbitrary"),
        name="gla_inproj",
    )(h, gain, w_in, colscale, w1cat, w2bd, gbcat)


def _chunk_cumsum(x):
    row = lax.broadcasted_iota(jnp.int32, x.shape, 0) % CHUNK
    s = 1
    while s < CHUNK:
        x = x + jnp.where(row >= s, pltpu.roll(x, s, axis=0), 0.0)
        s *= 2
    return x


def _col(v):
    n = v.shape[1]
    return jnp.transpose(jnp.broadcast_to(v, (SUBLANES, n)))[:, 0:1]


def _gla_scan_kernel(k0_ref, v0_ref, lgb0_ref, q_ref, k_ref, v_ref, r_ref, lgf_ref, lgb_ref, gain_ref, o_ref,
                     st_ref, fstate_ref, bstate_ref, *, n0):
    t = pl.program_id(2)
    dk = q_ref.shape[-1]
    nc0 = k0_ref.shape[1] // CHUNK
    nc1 = q_ref.shape[1] // CHUNK

    @pl.when(t == 0)
    def _():
        bstate_ref[...] = jnp.zeros_like(bstate_ref)
        fstate_ref[...] = jnp.zeros_like(fstate_ref)

    @pl.when(t < n0)
    def _():
        c0 = (n0 - 1 - t) * nc0
        lb = lgb0_ref[0]
        pb = _chunk_cumsum(lb)
        ke = (k0_ref[0].astype(F32) * jnp.exp(pb - lb)).astype(BF16)
        v = v0_ref[0]
        for ci in range(nc0 - 1, -1, -1):
            rs = slice(ci * CHUNK, (ci + 1) * CHUNK)
            st_ref[c0 + ci, dk:, :] = bstate_ref[...].astype(BF16)
            tot = pb[(ci + 1) * CHUNK - 1:(ci + 1) * CHUNK, :]
            bstate_ref[...] = bstate_ref[...] * _col(jnp.exp(tot)) + _dot_tn(ke[rs], v[rs])

    @pl.when(t >= n0)
    def _():
        c0 = (t - n0) * nc1
        q = q_ref[0].astype(F32)
        k = k_ref[0].astype(F32)
        v = v_ref[0]
        lf = lgf_ref[0]
        lb = lgb_ref[0]
        pf = _chunk_cumsum(lf)
        pb = _chunk_cumsum(lb)
        ii = lax.broadcasted_iota(jnp.int32, (CHUNK, CHUNK), 0)
        jj = lax.broadcasted_iota(jnp.int32, (CHUNK, CHUNK), 1)
        lower = ii >= jj
        for ci in range(nc1):
            rs = slice(ci * CHUNK, (ci + 1) * CHUNK)
            last = slice((ci + 1) * CHUNK - 1, (ci + 1) * CHUNK)
            pfc, pbc, lbc = pf[rs], pb[rs], lb[rs]
            totf, totb = pf[last], pb[last]
            cb = totb - pbc + lbc
            qc, kc, vc = q[rs], k[rs], v[rs]
            qf = (qc * jnp.exp(pfc)).astype(BF16)
            kf = (kc * jnp.exp(-pfc)).astype(BF16)
            qb = (qc * jnp.exp(cb)).astype(BF16)
            kb = (kc * jnp.exp(-cb)).astype(BF16)
            st_ref[c0 + ci, :dk, :] = fstate_ref[...].astype(BF16)
            p = jnp.where(lower, _dot_nt(qf, kf), _dot_nt(qb, kb)).astype(BF16)
            o = _dot(p, vc) + _dot(jnp.concatenate([qf, qb], axis=1), st_ref[c0 + ci])
            ke = (kc * jnp.exp(totf - pfc)).astype(BF16)
            fstate_ref[...] = fstate_ref[...] * _col(jnp.exp(totf)) + _dot_tn(ke, vc)
            o_ref[0, rs, :] = (_rms(o, gain_ref[...]) * _silu(r_ref[0, rs, :].astype(F32))).astype(BF16)


def _gla_scan(qkvr, lg, norm_gain, bsz, seq, heads, dk, dv):
    r0 = min(SCAN_ROWS_BWD, seq)
    r1 = min(SCAN_ROWS_FWD, seq)
    n0, n1 = seq // r0, seq // r1
    x = qkvr.reshape(bsz, seq, qkvr.shape[-1])
    lg3 = lg.reshape(bsz, seq, lg.shape[-1])
    blk0 = lambda t: jnp.maximum(n0 - 1 - t, 0)
    blk1 = lambda t: jnp.maximum(t - n0, 0)
    ko = heads * dk // dk
    vo = 2 * heads * dk // dv
    ro = (2 * heads * dk + heads * dv) // dv
    return pl.pallas_call(
        functools.partial(_gla_scan_kernel, n0=n0),
        grid=(bsz, heads, n0 + n1),
        in_specs=[pl.BlockSpec((1, r0, dk), lambda b, h, t: (b, blk0(t), ko + h)),
                  pl.BlockSpec((1, r0, dv), lambda b, h, t: (b, blk0(t), vo + h)),
                  pl.BlockSpec((1, r0, dk), lambda b, h, t: (b, blk0(t), heads + h)),
                  pl.BlockSpec((1, r1, dk), lambda b, h, t: (b, blk1(t), h)),
                  pl.BlockSpec((1, r1, dk), lambda b, h, t: (b, blk1(t), ko + h)),
                  pl.BlockSpec((1, r1, dv), lambda b, h, t: (b, blk1(t), vo + h)),
                  pl.BlockSpec((1, r1, dv), lambda b, h, t: (b, blk1(t), ro + h)),
                  pl.BlockSpec((1, r1, dk), lambda b, h, t: (b, blk1(t), h)),
                  pl.BlockSpec((1, r1, dk), lambda b, h, t: (b, blk1(t), heads + h)),
                  pl.BlockSpec((1, dv), lambda b, h, t: (0, 0))],
        out_specs=pl.BlockSpec((1, r1, dv), lambda b, h, t: (b, blk1(t), h)),
        out_shape=jax.ShapeDtypeStruct((bsz, seq, heads * dv), BF16),
        scratch_shapes=[pltpu.VMEM((seq // CHUNK, 2 * dk, dv), BF16),
                        pltpu.VMEM((dk, dv), F32),
                        pltpu.VMEM((dk, dv), F32)],
        compiler_params=_params("arbitrary", "arbitrary", "arbitrary"),
        name="gla_scan",
    )(x, x, lg3, x, x, x, x, lg3, lg3, norm_gain).reshape(bsz * seq, heads * dv)


def _mlp_kernel(hn_ref, hr_ref, g2_ref, wu_ref, wd_ref, g3_ref, o_ref,
                u0_ref, u1_ref, acc0_ref, acc1_ref, *, ni, ts):
    it = pl.program_id(0)
    c = pl.program_id(1)
    rows = pl.ds(pl.multiple_of(c * ts, ts), ts)
    u_refs = (u0_ref, u1_ref)
    acc_refs = (acc0_ref, acc1_ref)

    def prenorm(par):
        u_refs[par][rows, :] = _rms(hn_ref[...], g2_ref[...]).astype(BF16)

    def mlp_tile(par):
        acc_ref = acc_refs[1 - par]
        a = jnp.maximum(_dot(u_refs[1 - par][...], wu_ref[...]), 0.0)
        part = _dot((a * a).astype(BF16), wd_ref[...])
        acc_ref[...] = jnp.where(c == 0, 0.0, acc_ref[...]) + part

    def finish(par):
        o_ref[...] = hr_ref[...] + _rms(acc_refs[par][rows, :], g3_ref[...])

    @pl.when(it == 0)
    def _():
        @pl.when(c == 0)
        def _():
            acc0_ref[...] = jnp.zeros_like(acc0_ref)
            acc1_ref[...] = jnp.zeros_like(acc1_ref)
        prenorm(0)

    def step(par):
        prenorm(par)
        mlp_tile(par)
        finish(par)

    _by_parity(it, 1, ni, step)
    pl.when(it == ni + 1)(functools.partial(finish, (ni + 1) & 1))


def _mlp(h, g2, w_up, w_down, g3):
    t, d = h.shape
    ff = w_up.shape[1]
    nc = ROW_SPLIT
    tf = ff // nc
    tm = min(ROW_BLOCK, t)
    ni = t // tm
    ts = tm // nc
    return pl.pallas_call(
        functools.partial(_mlp_kernel, ni=ni, ts=ts),
        grid=(ni + 2, nc),
        in_specs=[pl.BlockSpec((ts, d), lambda it, c: (jnp.minimum(it, ni - 1) * nc + c, 0)),
                  pl.BlockSpec((ts, d), lambda it, c: (jnp.maximum(it - 2, 0) * nc + c, 0)),
                  pl.BlockSpec((1, d), lambda it, c: (0, 0)),
                  pl.BlockSpec((d, tf), lambda it, c: (0, c)),
                  pl.BlockSpec((tf, d), lambda it, c: (c, 0)),
                  pl.BlockSpec((1, d), lambda it, c: (0, 0))],
        out_specs=pl.BlockSpec((ts, d), lambda it, c: (jnp.where(it < 2, 0, (it - 2) * nc + c), 0)),
        out_shape=jax.ShapeDtypeStruct((t, d), F32),
        scratch_shapes=[pltpu.VMEM((tm, d), BF16), pltpu.VMEM((tm, d), BF16),
                        pltpu.VMEM((tm, d), F32), pltpu.VMEM((tm, d), F32)],
        compiler_params=_params("arbitrary", "arbitrary"),
        name="mlp",
    )(h, h, g2, w_up, w_down, g3)


def _head_major(w, heads, dk, dv):
    d = w.shape[0]
    bounds = (0, heads * dk, 2 * heads * dk, 2 * heads * dk + heads * dv, w.shape[1])
    parts = [w[:, lo:hi].reshape(d, heads, -1) for lo, hi in zip(bounds[:-1], bounds[1:])]
    return jnp.concatenate(parts, axis=2).reshape(d, -1)


def _retention_layer(h, gains, cos, sin, w_in, decay_logit, w_out, bsz, seq):
    heads = RET_HEADS
    dk = w_out.shape[1] // heads
    dv = w_out.shape[0] // heads
    w_heads = _head_major(w_in, heads, dk, dv).astype(BF16)
    qkvg = _ret_inproj(h, gains[0:1], w_heads, cos, sin, heads, dk)
    o = _ret_scan(qkvg, decay_logit, bsz, seq, heads, dk, dv)
    return _outproj(o, w_out.astype(BF16), gains[1:2], h)


def _conv_layer(h, gains, w_in, b_in, w_dw, b_dw, ln_g, ln_b, w_out, b_out, bsz, seq):
    row = lambda a: a.reshape(1, -1)
    h1 = _conv_inproj(h, gains[0:1], w_in.astype(BF16), row(b_in))
    return _conv_mix(h1, w_dw, row(b_dw), row(ln_g), row(ln_b), w_out.astype(BF16), row(b_out),
                     gains[1:2], h, bsz, seq)


def _gla_layer(h, gains, w_in, gate_w1, gate_w2, gate_b, norm_gain, w_out, bsz, seq):
    heads = GLA_HEADS
    dv = w_out.shape[0] // heads
    rank = gate_w1.shape[-1]
    nk = gate_w2.shape[-1]
    dk = nk // heads
    w1cat = jnp.concatenate([gate_w1[0], gate_w1[1]], axis=1).astype(BF16)
    zeros = jnp.zeros((rank, nk), gate_w2.dtype)
    w2bd = jnp.concatenate([jnp.concatenate([gate_w2[0], zeros], axis=1),
                            jnp.concatenate([zeros, gate_w2[1]], axis=1)], axis=0).astype(BF16)
    gbcat = gate_b.reshape(1, 2 * nk)
    colscale = jnp.where(jnp.arange(w_in.shape[1]) < nk, dk ** -0.5, 1.0).astype(F32).reshape(1, -1)
    qkvr, lg = _gla_inproj(h, gains[0:1], w_in.astype(BF16), colscale, w1cat, w2bd, gbcat)
    o = _gla_scan(qkvr, lg, norm_gain.reshape(1, dv), bsz, seq, heads, dk, dv)
    return _outproj(o, w_out.astype(BF16), gains[1:2], h)


def kernel(x, positions, norm_gains, ret_w_in, ret_decay_logit, ret_w_out, conv_w_in, conv_b_in, conv_w_dw, conv_b_dw, conv_ln_gain, conv_ln_bias, conv_w_out, conv_b_out, gla_w_in, gla_gate_w1, gla_gate_w2, gla_gate_b, gla_norm_gain, gla_w_out, mlp_w_up, mlp_w_down):
    bsz, seq, d = x.shape
    depth = norm_gains.shape[0]
    h = x.reshape(bsz * seq, d)
    ret_dk = ret_w_out.shape[2] // RET_HEADS
    cos, sin = _rope_tables(positions, ret_dk // 2)
    for i in range(depth):
        kind, j = i % 3, i // 3
        gains = norm_gains[i]
        if kind == 0:
            h = _retention_layer(h, gains, cos, sin, ret_w_in[j], ret_decay_logit[j], ret_w_out[j], bsz, seq)
        elif kind == 1:
            h = _conv_layer(h, gains, conv_w_in[j], conv_b_in[j], conv_w_dw[j], conv_b_dw[j],
                            conv_ln_gain[j], conv_ln_bias[j], conv_w_out[j], conv_b_out[j], bsz, seq)
        else:
            h = _gla_layer(h, gains, gla_w_in[j], gla_gate_w1[j], gla_gate_w2[j], gla_gate_b[j],
                           gla_norm_gain[j], gla_w_out[j], bsz, seq)
        h = _mlp(h, gains[2:3], mlp_w_up[i].astype(BF16), mlp_w_down[i].astype(BF16), gains[3:4])
    return h.reshape(bsz, seq, d)
```
